```python
import math
import jax, jax.numpy as jnp
from jax import lax
import numpy as np

D_MODEL = 1024
BATCH = 4
SEQ = 4096
DEPTH = 4

CHUNK = 64
MEM_LEN = 256
HEAD_DIM = 64
SB_HEADS = 8
SB_WIDTH = SB_HEADS * HEAD_DIM
POOL_WINDOWS = (2, 4, 8, 16)
POOL_GROUPS = 4
POOL_GROUP_DIM = 64
POOL_WIDTH = POOL_GROUPS * POOL_GROUP_DIM
X_HEADS = 4
X_WIDTH = X_HEADS * HEAD_DIM
N_BRANCH = 3
IN_COLS = 3 * SB_WIDTH + POOL_WIDTH + X_WIDTH + N_BRANCH * D_MODEL
Q_BLOCK = 128
N_GROUPS = 4
EXPERTS_PER_GROUP = 4
N_EXPERTS = N_GROUPS * EXPERTS_PER_GROUP
TOP_K = 2
EXPERT_FF = 512
EPS = 1e-6

kernel_name = "stickbreak_pool_memxattn_hiermoe_trunk"


def rmsnorm(x, g):
    xf = x.astype(jnp.float32)
    y = xf * lax.rsqrt(jnp.mean(xf * xf, axis=-1, keepdims=True) + EPS)
    return (y * g.astype(jnp.float32)).astype(x.dtype)


def stick_breaking_attention(q, k, v):
    B, H, S, Dh = q.shape
    nb = S // Q_BLOCK
    scale = Dh ** -0.5
    qb = q.reshape(B, H, nb, Q_BLOCK, Dh).transpose(2, 0, 1, 3, 4)
    key_pos = jnp.arange(S)

    def block(args):
        qi, i = args
        z = jnp.einsum('bhqd,bhkd->bhqk', qi, k).astype(jnp.float32) * scale
        q_pos = i * Q_BLOCK + jnp.arange(Q_BLOCK)
        strict = key_pos[None, :] < q_pos[:, None]
        log_1m = jnp.where(strict, jax.nn.log_sigmoid(-z), 0.0)
        tail = lax.cumsum(log_1m, axis=3, reverse=True) - log_1m
        a = jnp.where(strict, jnp.exp(jax.nn.log_sigmoid(z) + tail), 0.0)
        return jnp.einsum('bhqk,bhkd->bhqd', a.astype(v.dtype), v)

    out = lax.map(block, (qb, jnp.arange(nb)))
    return out.transpose(1, 2, 0, 3, 4).reshape(B, H, S, Dh)


def multiscale_pool(u, w_pool, pool_scale):
    B, S, _ = u.shape
    ug = u.reshape(B, S, POOL_GROUPS, POOL_GROUP_DIM)
    pos = jnp.arange(1, S + 1, dtype=jnp.float32)
    outs = []
    for g, w in enumerate(POOL_WINDOWS):
        ch = ug[:, :, g].astype(jnp.float32)
        c = jnp.cumsum(ch, axis=1)
        c_lag = jnp.pad(c, ((0, 0), (w, 0), (0, 0)))[:, :S]
        mean = (c - c_lag) / jnp.minimum(pos, w)[None, :, None]
        outs.append(mean - ch)
    d = jnp.stack(outs, axis=2).astype(u.dtype)
    y = jnp.einsum('bsgc,gcd->bsgd', d, w_pool)
    return y.reshape(B, S, POOL_WIDTH) * pool_scale


def memory_cross_attention(qx, mem_n, w_kv_mem):
    B, S, _ = qx.shape
    M = mem_n.shape[1]
    q = qx.reshape(B, S, X_HEADS, HEAD_DIM)
    kv = mem_n @ w_kv_mem
    k = kv[..., :X_WIDTH].reshape(B, M, X_HEADS, HEAD_DIM)
    v = kv[..., X_WIDTH:].reshape(B, M, X_HEADS, HEAD_DIM)
    s = jnp.einsum('bqhd,bmhd->bhqm', q, k).astype(jnp.float32) * (HEAD_DIM ** -0.5)
    p = jax.nn.softmax(s, axis=-1).astype(v.dtype)
    o = jnp.einsum('bhqm,bmhd->bqhd', p, v)
    return o.reshape(B, S, X_WIDTH)


def hybrid_mixer(h, mem_n, w_in, w_pool, pool_scale, w_kv_mem, w_up_sb, w_up_pool, w_up_x, w_out):
    B, S, _ = h.shape
    z = h @ w_in
    o = 0
    q = z[..., o:o + SB_WIDTH]; o += SB_WIDTH
    k = z[..., o:o + SB_WIDTH]; o += SB_WIDTH
    v = z[..., o:o + SB_WIDTH]; o += SB_WIDTH
    u = z[..., o:o + POOL_WIDTH]; o += POOL_WIDTH
    qx = z[..., o:o + X_WIDTH]; o += X_WIDTH
    g = z[..., o:o + N_BRANCH * D_MODEL]

    def heads(t):
        return t.reshape(B, S, SB_HEADS, HEAD_DIM).transpose(0, 2, 1, 3)

    sb = stick_breaking_attention(heads(q), heads(k), heads(v))
    sb = sb.transpose(0, 2, 1, 3).reshape(B, S, SB_WIDTH)
    pl = multiscale_pool(u, w_pool, pool_scale)
    xa = memory_cross_attention(qx, mem_n, w_kv_mem)

    gates = jax.nn.sigmoid(g.astype(jnp.float32)).astype(h.dtype).reshape(B, S, N_BRANCH, D_MODEL)
    merged = (gates[:, :, 0] * (sb @ w_up_sb)
              + gates[:, :, 1] * (pl @ w_up_pool)
              + gates[:, :, 2] * (xa @ w_up_x))
    return merged @ w_out


def hierarchical_moe(h, w_group, b_group, w_router, b_router, w_gate, w_up, w_down):
    B, S, D = h.shape
    t = h.reshape(-1, D)
    T = t.shape[0]
    gl = (t @ w_group).astype(jnp.float32) + b_group.astype(jnp.float32)
    gp = jax.nn.softmax(gl, axis=-1)
    g_idx = jnp.argmax(gl, axis=-1)
    g_prob = jnp.take_along_axis(gp, g_idx[:, None], axis=1)[:, 0]
    el = ((t @ w_router).astype(jnp.float32) + b_router.astype(jnp.float32)).reshape(T, N_GROUPS, EXPERTS_PER_GROUP)
    el_sel = jnp.take_along_axis(el, g_idx[:, None, None], axis=1)[:, 0]
    top_v, top_i = lax.top_k(el_sel, TOP_K)
    wts = jax.nn.softmax(top_v, axis=-1) * g_prob[:, None]
    e_idx = g_idx[:, None] * EXPERTS_PER_GROUP + top_i
    gate = jnp.sum(jax.nn.one_hot(e_idx, N_EXPERTS, dtype=jnp.float32) * wts[..., None], axis=1)
    gate = gate.astype(t.dtype)
    out = jnp.zeros_like(t)
    for e in range(N_EXPERTS):
        a = jax.nn.silu(t @ w_gate[e]) * (t @ w_up[e])
        out = out + gate[:, e:e + 1] * (a @ w_down[e])
    return out.reshape(B, S, D)


def setup_inputs(seed: int = 0) -> dict:
    key = jax.random.key(seed)
    ks = jax.random.split(key, 24)
    f = jnp.float32
    L, D = DEPTH, D_MODEL

    def nrm(k, shape, fan_in):
        return jax.random.normal(k, shape, f) * (fan_in ** -0.5)

    def gain(k, shape):
        return 1.0 + 0.02 * jax.random.normal(k, shape, f)

    return {
        "x": jax.random.normal(ks[0], (BATCH, SEQ, D), f),
        "mem": jax.random.normal(ks[1], (BATCH, MEM_LEN, D), f),
        "norm_mix": gain(ks[2], (L, D)),
        "norm_mem": gain(ks[3], (L, D)),
        "w_in": nrm(ks[4], (L, D, IN_COLS), D),
        "w_pool": nrm(ks[5], (L, POOL_GROUPS, POOL_GROUP_DIM, POOL_GROUP_DIM), POOL_GROUP_DIM),
        "pool_scale": gain(ks[6], (L, POOL_WIDTH)),
        "w_kv_mem": nrm(ks[7], (L, D, 2 * X_WIDTH), D),
        "w_up_sb": nrm(ks[8], (L, SB_WIDTH, D), SB_WIDTH),
        "w_up_pool": nrm(ks[9], (L, POOL_WIDTH, D), POOL_WIDTH),
        "w_up_x": nrm(ks[10], (L, X_WIDTH, D), X_WIDTH),
        "w_out": nrm(ks[11], (L, D, D), D),
        "norm_ffn": gain(ks[12], (L, D)),
        "w_group": nrm(ks[13], (L, D, N_GROUPS), D),
        "b_group": 0.01 * jax.random.normal(ks[14], (L, N_GROUPS), f),
        "w_router": nrm(ks[15], (L, D, N_EXPERTS), D),
        "b_router": 0.01 * jax.random.normal(ks[16], (L, N_EXPERTS), f),
        "w_gate": nrm(ks[17], (L, N_EXPERTS, D, EXPERT_FF), D),
        "w_up": nrm(ks[18], (L, N_EXPERTS, D, EXPERT_FF), D),
        "w_down": nrm(ks[19], (L, N_EXPERTS, EXPERT_FF, D), EXPERT_FF),
        "norm_final": gain(ks[20], (D,)),
    }


def reference(x, mem, norm_mix, norm_mem, w_in, w_pool, pool_scale, w_kv_mem, w_up_sb, w_up_pool,
              w_up_x, w_out, norm_ffn, w_group, b_group, w_router, b_router, w_gate, w_up, w_down,
              norm_final):
    for l in range(DEPTH):
        h = rmsnorm(x, norm_mix[l])
        mem_n = rmsnorm(mem, norm_mem[l])
        x = x + hybrid_mixer(h, mem_n, w_in[l], w_pool[l], pool_scale[l], w_kv_mem[l],
                             w_up_sb[l], w_up_pool[l], w_up_x[l], w_out[l])
        h = rmsnorm(x, norm_ffn[l])
        x = x + hierarchical_moe(h, w_group[l], b_group[l], w_router[l], b_router[l],
                                 w_gate[l], w_up[l], w_down[l])
    return rmsnorm(x, norm_final)
```

```python
import functools
import math

import jax
import jax.numpy as jnp
import numpy as np
from jax import lax
from jax.experimental import pallas as pl
from jax.experimental.pallas import tpu as pltpu

F32 = jnp.float32
BF16 = jnp.bfloat16

D_MODEL = 1024
HEAD_DIM = 64
SB_HEADS = 8
SB_WIDTH = SB_HEADS * HEAD_DIM
POOL_WINDOWS = (2, 4, 8, 16)
POOL_GROUP_DIM = 64
POOL_WIDTH = 256
X_HEADS = 4
X_WIDTH = X_HEADS * HEAD_DIM
N_BRANCH = 3
IN_COLS = 3 * SB_WIDTH + POOL_WIDTH + X_WIDTH + N_BRANCH * D_MODEL
N_GROUPS = 4
EXPERTS_PER_GROUP = 4
N_EXPERTS = N_GROUPS * EXPERTS_PER_GROUP
EXPERT_FF = 512
EPS = 1e-6

LANES = 128
SUBLANES = 8
VMEM_LIMIT = 56 * 1024 * 1024

TM_PROJ = 1024
TN_PROJ = 1024
T_ATT = 256
TM_MERGE = 256
TM_MOE = 256
TM_COMB = 256
TC_POS = 512
HALO = 16

LOG2E = 1.4426950408889634
INV_LN2 = LOG2E
NEG_BIG = -1e30
ROUTE_LANES = LANES


def _cparams(sem):
    return pltpu.CompilerParams(dimension_semantics=sem, vmem_limit_bytes=VMEM_LIMIT)


def _rms(x, g):
    ms = jnp.mean(x * x, axis=-1, keepdims=True)
    return x * lax.rsqrt(ms + EPS) * g


def _memkv_kernel(mem_ref, g_ref, w_ref, kv_ref):
    m = _rms(mem_ref[0], g_ref[0]).astype(BF16)
    kv_ref[0, 0] = jnp.dot(m, w_ref[0].astype(BF16), preferred_element_type=F32).astype(BF16)


def _memkv(mem, norm_mem, w_kv_mem):
    L = norm_mem.shape[0]
    B, M, D = mem.shape
    return pl.pallas_call(
        _memkv_kernel,
        grid=(L, B),
        in_specs=[
            pl.BlockSpec((1, M, D), lambda l, b: (b, 0, 0)),
            pl.BlockSpec((1, 1, D), lambda l, b: (l, 0, 0)),
            pl.BlockSpec((1, D, 2 * X_WIDTH), lambda l, b: (l, 0, 0)),
        ],
        out_specs=pl.BlockSpec((1, 1, M, 2 * X_WIDTH), lambda l, b: (l, b, 0, 0)),
        out_shape=jax.ShapeDtypeStruct((L, B, M, 2 * X_WIDTH), BF16),
        compiler_params=_cparams(("arbitrary", "arbitrary")),
        name="memkv",
    )(mem, norm_mem.reshape(L, 1, D), w_kv_mem)


def _inproj_kernel(x_ref, g_ref, w_ref, cs_ref, z_ref, h_ref):
    @pl.when(pl.program_id(1) == 0)
    def _():
        h_ref[...] = _rms(x_ref[...], g_ref[...]).astype(BF16)

    acc = jnp.dot(h_ref[...], w_ref[...].astype(BF16), preferred_element_type=F32)
    z_ref[...] = (acc * cs_ref[...]).astype(BF16)


def _inproj(x, g, w, colscale):
    T, D = x.shape
    N = w.shape[1]
    return pl.pallas_call(
        _inproj_kernel,
        grid=(T // TM_PROJ, N // TN_PROJ),
        in_specs=[
            pl.BlockSpec((TM_PROJ, D), lambda i, j: (i, 0)),
            pl.BlockSpec((1, D), lambda i, j: (0, 0)),
            pl.BlockSpec((D, TN_PROJ), lambda i, j: (0, j)),
            pl.BlockSpec((1, TN_PROJ), lambda i, j: (0, j)),
        ],
        out_specs=pl.BlockSpec((TM_PROJ, TN_PROJ), lambda i, j: (i, j)),
        out_shape=jax.ShapeDtypeStruct((T, N), BF16),
        scratch_shapes=[pltpu.VMEM((TM_PROJ, D), BF16)],
        compiler_params=_cparams(("arbitrary", "arbitrary")),
        name="inproj",
    )(x, g.reshape(1, D), w, colscale)


def _sb_attn_kernel(q_ref, k_ref, v_ref, tri_ref, o_ref, acc_ref, c_ref):
    S = q_ref.shape[0]
    T = T_ATT
    nq = S // T
    lane = lax.broadcasted_iota(jnp.int32, (T, LANES), 1)
    row = lax.broadcasted_iota(jnp.int32, (2 * T, T), 0)
    col = lax.broadcasted_iota(jnp.int32, (2 * T, T), 1)
    strict = col < jnp.where(row >= T, row - T, row)
    tri = tri_ref[...]

    def q_block(qi, _):
        q = q_ref[pl.ds(pl.multiple_of(qi * T, T), T), :]
        zero = jnp.zeros_like(q)
        q2 = jnp.concatenate([jnp.where(lane < HEAD_DIM, q, zero), jnp.where(lane >= HEAD_DIM, q, zero)], axis=0)

        def scores(kb):
            kblk = k_ref[pl.ds(pl.multiple_of(kb * T, T), T), :]
            z2 = lax.dot_general(q2, kblk, (((1,), (1,)), ((), ())), preferred_element_type=F32)
            u = jnp.exp2(jnp.minimum(z2, 43.0))
            sp = jnp.maximum(jnp.log(1.0 + u) * INV_LN2, z2)
            return z2, sp

        z2, sp = scores(qi)
        l = jnp.where(strict, sp, 0.0).astype(BF16)
        cum = jnp.dot(l, tri, preferred_element_type=F32)
        a = jnp.where(strict, jnp.exp2(z2 - sp - cum), 0.0).astype(BF16)
        vblk = v_ref[pl.ds(pl.multiple_of(qi * T, T), T), :]
        acc_ref[...] = jnp.dot(a, vblk, preferred_element_type=F32)
        c_ref[...] = cum[:, 0:1] + l[:, 0:1].astype(F32)

        def k_block(i, _):
            kb = qi - 1 - i
            z2, sp = scores(kb)
            l = sp.astype(BF16)
            cum = jnp.dot(l, tri, preferred_element_type=F32)
            c = c_ref[...]
            a = jnp.exp2(z2 - sp - cum - c).astype(BF16)
            vblk = v_ref[pl.ds(pl.multiple_of(kb * T, T), T), :]
            acc_ref[...] += jnp.dot(a, vblk, preferred_element_type=F32)
            c_ref[...] = c + cum[:, 0:1] + l[:, 0:1].astype(F32)
            return 0

        lax.fori_loop(0, qi, k_block, 0)
        acc = acc_ref[...]
        o_ref[pl.ds(pl.multiple_of(qi * T, T), T), :] = jnp.where(lane < HEAD_DIM, acc[:T], acc[T:]).astype(o_ref.dtype)
        return 0

    lax.fori_loop(0, nq, q_block, 0)


def _sb_attention(z, tri, batch, seq):
    npair = SB_WIDTH // LANES
    return pl.pallas_call(
        _sb_attn_kernel,
        grid=(batch, npair),
        in_specs=[
            pl.BlockSpec((seq, LANES), lambda b, p: (b, p)),
            pl.BlockSpec((seq, LANES), lambda b, p: (b, npair + p)),
            pl.BlockSpec((seq, LANES), lambda b, p: (b, 2 * npair + p)),
            pl.BlockSpec((T_ATT, T_ATT), lambda b, p: (0, 0)),
        ],
        out_specs=pl.BlockSpec((seq, LANES), lambda b, p: (b, p)),
        out_shape=jax.ShapeDtypeStruct((batch * seq, SB_WIDTH), BF16),
        scratch_shapes=[pltpu.VMEM((2 * T_ATT, LANES), F32), pltpu.VMEM((2 * T_ATT, 1), F32)],
        compiler_params=_cparams(("arbitrary", "arbitrary")),
        name="sb_attention",
    )(z, z, z, tri)


def _merge_kernel(
    x_ref, u_ref, up_ref, qx_ref, g0_ref, g1_ref, g2_ref, sb_ref, kv_ref,
    wpool_ref, pscale_ref, wsb_ref, wpl_ref, wxa_ref, wout_ref, nffn_ref, wrh_ref, wrl_ref, brt_ref,
    xo_ref, ht_ref, route_ref, ubuf_ref, *, tiles_per_seq):
    TM = TM_MERGE
    i = pl.program_id(0)
    first = (i % tiles_per_seq) == 0

    u = u_ref[...].astype(F32)
    hist = up_ref[TM - HALO:, :].astype(F32)
    ubuf_ref[0:HALO, :] = hist * jnp.where(first, 0.0, 1.0)
    ubuf_ref[HALO:, :] = u
    lane_p = lax.broadcasted_iota(jnp.int32, (TM, POOL_WIDTH), 1)
    pos = (lax.broadcasted_iota(jnp.int32, (TM, POOL_WIDTH), 0) + (i % tiles_per_seq) * TM + 1).astype(F32)

    def shifted(k):
        return ubuf_ref[HALO - k:HALO - k + TM, :]

    s2 = u + shifted(1)
    s4 = s2 + shifted(2) + shifted(3)
    s8 = s4
    for k in range(4, 8):
        s8 = s8 + shifted(k)
    s16 = s8
    for k in range(8, 16):
        s16 = s16 + shifted(k)
    grp = lax.shift_right_logical(lane_p, int(math.log2(POOL_GROUP_DIM)))
    wsum = jnp.where(grp == 0, s2, jnp.where(grp == 1, s4, jnp.where(grp == 2, s8, s16)))
    win = jnp.where(grp == 0, 2.0, jnp.where(grp == 1, 4.0, jnp.where(grp == 2, 8.0, 16.0)))
    d = wsum / jnp.minimum(pos, win) - u
    yp = jnp.dot(d.astype(BF16), wpool_ref[...], preferred_element_type=F32) * pscale_ref[...]
    pl_up = jnp.dot(yp.astype(BF16), wpl_ref[...], preferred_element_type=F32)

    qx = qx_ref[...]
    kv = kv_ref[0, 0]
    kmem = kv[:, :X_WIDTH]
    vmem = kv[:, X_WIDTH:]
    lane_x = lax.broadcasted_iota(jnp.int32, (TM, X_WIDTH), 1)
    xa = jnp.zeros((TM, X_WIDTH), F32)
    for h in range(X_HEADS):
        hm = (lane_x >= h * HEAD_DIM) & (lane_x < (h + 1) * HEAD_DIM)
        qh = jnp.where(hm, qx, jnp.zeros_like(qx))
        s = lax.dot_general(qh, kmem, (((1,), (1,)), ((), ())), preferred_element_type=F32)
        m = jnp.max(s, axis=-1, keepdims=True)
        p = jnp.exp(s - m)
        den = jnp.sum(p, axis=-1, keepdims=True)
        o = jnp.dot(p.astype(BF16), vmem, preferred_element_type=F32)
        xa = jnp.where(hm, o / den, xa)
    xa_up = jnp.dot(xa.astype(BF16), wxa_ref[...], preferred_element_type=F32)

    sb_up = jnp.dot(sb_ref[...], wsb_ref[...], preferred_element_type=F32)
    merged = (jax.nn.sigmoid(g0_ref[...].astype(F32)) * sb_up
              + jax.nn.sigmoid(g1_ref[...].astype(F32)) * pl_up
              + jax.nn.sigmoid(g2_ref[...].astype(F32)) * xa_up)
    xn = x_ref[...] + jnp.dot(merged.astype(BF16), wout_ref[...], preferred_element_type=F32)
    xo_ref[...] = xn

    h2 = _rms(xn, nffn_ref[...])
    for s in range(D_MODEL // LANES):
        ht_ref[pl.ds(s, TM, stride=SUBLANES), :] = h2[:, s * LANES:(s + 1) * LANES]

    hh = h2.astype(BF16)
    hl = (h2 - hh.astype(F32)).astype(BF16)
    wrh = wrh_ref[...]
    logits = (jnp.dot(hh, wrh, preferred_element_type=F32)
              + jnp.dot(hl, wrh, preferred_element_type=F32)
              + jnp.dot(hh, wrl_ref[...], preferred_element_type=F32)) + brt_ref[...]
    lane = lax.broadcasted_iota(jnp.int32, (TM, ROUTE_LANES), 1)
    lanef = lane.astype(F32)
    big = float(ROUTE_LANES)
    glm = jnp.where(lane < N_GROUPS, logits, NEG_BIG)
    gmax = jnp.max(glm, axis=-1, keepdims=True)
    gidx = jnp.min(jnp.where(glm == gmax, lanef, big), axis=-1, keepdims=True)
    gprob = 1.0 / jnp.sum(jnp.exp(glm - gmax), axis=-1, keepdims=True)
    lo = N_GROUPS + EXPERTS_PER_GROUP * gidx
    elm = jnp.where((lanef >= lo) & (lanef < lo + EXPERTS_PER_GROUP), logits, NEG_BIG)
    v1 = jnp.max(elm, axis=-1, keepdims=True)
    i1 = jnp.min(jnp.where(elm == v1, lanef, big), axis=-1, keepdims=True)
    elm2 = jnp.where(lanef == i1, NEG_BIG, elm)
    v2 = jnp.max(elm2, axis=-1, keepdims=True)
    i2 = jnp.min(jnp.where(elm2 == v2, lanef, big), axis=-1, keepdims=True)
    e21 = jnp.exp(v2 - v1)
    w1 = gprob / (1.0 + e21)
    w2 = gprob * e21 / (1.0 + e21)
    route_ref[...] = jnp.where(lane == 0, w1, jnp.where(lane == 1, w2, jnp.where(
        lane == 2, i1 - N_GROUPS, jnp.where(lane == 3, i2 - N_GROUPS, 0.0))))


def _merge(x, z, sb, kv, layer, wts, batch, seq):
    T, D = x.shape
    TM = TM_MERGE
    tiles_per_seq = seq // TM
    ucol = (3 * SB_WIDTH) // POOL_WIDTH
    qxcol = (3 * SB_WIDTH + POOL_WIDTH) // X_WIDTH
    gcol = (3 * SB_WIDTH + POOL_WIDTH + X_WIDTH) // D_MODEL

    def const(shape):
        return pl.BlockSpec(shape, lambda i: (0,) * len(shape))

    kernel = functools.partial(_merge_kernel, tiles_per_seq=tiles_per_seq)
    return pl.pallas_call(
        kernel,
        grid=(T // TM,),
        in_specs=[
            pl.BlockSpec((TM, D), lambda i: (i, 0)),
            pl.BlockSpec((TM, POOL_WIDTH), lambda i: (i, ucol)),
            pl.BlockSpec((TM, POOL_WIDTH), lambda i: (jnp.maximum(i - 1, 0), ucol)),
            pl.BlockSpec((TM, X_WIDTH), lambda i: (i, qxcol)),
            pl.BlockSpec((TM, D), lambda i: (i, gcol)),
            pl.BlockSpec((TM, D), lambda i: (i, gcol + 1)),
            pl.BlockSpec((TM, D), lambda i: (i, gcol + 2)),
            pl.BlockSpec((TM, SB_WIDTH), lambda i: (i, 0)),
            pl.BlockSpec((1, 1) + kv.shape[2:], lambda i: (layer, i // tiles_per_seq, 0, 0)),
            const((POOL_WIDTH, POOL_WIDTH)),
            const((1, POOL_WIDTH)),
            const((SB_WIDTH, D)),
            const((POOL_WIDTH, D)),
            const((X_WIDTH, D)),
            const((D, D)),
            const((1, D)),
            const((D, ROUTE_LANES)),
            const((D, ROUTE_LANES)),
            const((1, ROUTE_LANES)),
        ],
        out_specs=[
            pl.BlockSpec((TM, D), lambda i: (i, 0)),
            pl.BlockSpec((TM * SUBLANES, LANES), lambda i: (i, 0)),
            pl.BlockSpec((TM, ROUTE_LANES), lambda i: (i, 0)),
        ],
        out_shape=[
            jax.ShapeDtypeStruct((T, D), F32),
            jax.ShapeDtypeStruct((T * SUBLANES, LANES), F32),
            jax.ShapeDtypeStruct((T, ROUTE_LANES), F32),
        ],
        scratch_shapes=[pltpu.VMEM((HALO + TM, POOL_WIDTH), F32)],
        compiler_params=_cparams(("arbitrary",)),
        name="merge",
    )(x, z, z, z, z, z, z, sb, kv, *wts)


def _pos_kernel(e_ref, tri_ref, pos_ref, tile_ref, cb_ref, carry_ref):
    T = e_ref.shape[1]
    TC = TC_POS
    nchunk = T // TC
    eid = lax.broadcasted_iota(jnp.int32, (N_EXPERTS, TC), 0)
    tri = tri_ref[...]
    carry_ref[...] = jnp.zeros_like(carry_ref)

    def rank_chunk(c, _):
        off = pl.multiple_of(c * TC, TC)
        e0 = e_ref[0:1, pl.ds(off, TC)]
        e1 = e_ref[1:2, pl.ds(off, TC)]
        m0 = eid == e0
        m1 = eid == e1
        m = jnp.where(m0 | m1, 1.0, 0.0)
        cum = jnp.dot(m.astype(BF16), tri, preferred_element_type=F32) + carry_ref[...]
        r0 = jnp.sum(jnp.where(m0, cum, 0.0), axis=0, keepdims=True)
        r1 = jnp.sum(jnp.where(m1, cum, 0.0), axis=0, keepdims=True)
        pos_ref[0:1, pl.ds(off, TC)] = r0.astype(jnp.int32)
        pos_ref[1:2, pl.ds(off, TC)] = r1.astype(jnp.int32)
        carry_ref[...] = carry_ref[...] + jnp.sum(m, axis=1, keepdims=True)
        return 0

    lax.fori_loop(0, nchunk, rank_chunk, 0)

    cnt = carry_ref[...]
    ntile = jnp.floor((cnt + (TM_MOE - 1)) * (1.0 / TM_MOE))
    er = lax.broadcasted_iota(jnp.int32, (N_EXPERTS, N_EXPERTS), 0)
    ec = lax.broadcasted_iota(jnp.int32, (N_EXPERTS, N_EXPERTS), 1)
    lower = jnp.where(ec < er, 1.0, 0.0).astype(BF16)
    ntile_b = jnp.broadcast_to(ntile, (N_EXPERTS, LANES)).astype(BF16)
    base_t = jnp.dot(lower, ntile_b, preferred_element_type=F32)
    base = base_t[:, 0:1] * float(TM_MOE)
    end_t = base_t[:, 0:1] + ntile

    def add_base(c, _):
        off = pl.multiple_of(c * TC, TC)
        e0 = e_ref[0:1, pl.ds(off, TC)]
        e1 = e_ref[1:2, pl.ds(off, TC)]
        b0 = jnp.sum(jnp.where(eid == e0, base, 0.0), axis=0, keepdims=True)
        b1 = jnp.sum(jnp.where(eid == e1, base, 0.0), axis=0, keepdims=True)
        pos_ref[0:1, pl.ds(off, TC)] = pos_ref[0:1, pl.ds(off, TC)] + b0.astype(jnp.int32)
        pos_ref[1:2, pl.ds(off, TC)] = pos_ref[1:2, pl.ds(off, TC)] + b1.astype(jnp.int32)
        return 0

    lax.fori_loop(0, nchunk, add_base, 0)

    ntl = tile_ref.shape[1]
    tidx = lax.broadcasted_iota(jnp.int32, (N_EXPERTS, ntl), 1).astype(F32)
    texp = jnp.sum(jnp.where(tidx >= end_t, 1.0, 0.0), axis=0, keepdims=True)
    total = jnp.max(end_t, axis=0, keepdims=True)
    rowi = lax.broadcasted_iota(jnp.int32, tile_ref.shape, 0)
    tile_ref[...] = jnp.where(rowi == 0, jnp.minimum(texp, float(N_EXPERTS - 1)), total).astype(jnp.int32)
    lane = lax.broadcasted_iota(jnp.int32, cb_ref.shape, 1)
    cb_ref[...] = jnp.where(lane == 0, base, jnp.where(lane == 1, cnt, 0.0)).astype(jnp.int32)


def _positions(e_idx, tri, ntl):
    T = e_idx.shape[1]
    return pl.pallas_call(
        _pos_kernel,
        grid=(1,),
        in_specs=[
            pl.BlockSpec((2, T), lambda i: (0, 0)),
            pl.BlockSpec((TC_POS, TC_POS), lambda i: (0, 0)),
        ],
        out_specs=[
            pl.BlockSpec((2, T), lambda i: (0, 0)),
            pl.BlockSpec((SUBLANES, ntl), lambda i: (0, 0)),
            pl.BlockSpec((N_EXPERTS, LANES), lambda i: (0, 0)),
        ],
        out_shape=[
            jax.ShapeDtypeStruct((2, T), jnp.int32),
            jax.ShapeDtypeStruct((SUBLANES, ntl), jnp.int32),
            jax.ShapeDtypeStruct((N_EXPERTS, LANES), jnp.int32),
        ],
        scratch_shapes=[pltpu.VMEM((N_EXPERTS, 1), F32)],
        compiler_params=_cparams(("arbitrary",)),
        name="positions",
    )(e_idx, tri)


DISPATCH_CHUNK = 256


def _dispatch_kernel(pos_ref, base_ref, cnt_ref, nt_ref, h_ref, xs_ref, zero_ref, sem, zsem):
    T = h_ref.shape[0] // SUBLANES
    CH = DISPATCH_CHUNK
    nchunk = T // CH
    tile_rows = TM_MOE * SUBLANES
    ntl = xs_ref.shape[0] // tile_rows

    def row_copy(t, slot):
        return pltpu.make_async_copy(
            h_ref.at[pl.ds(pl.multiple_of(t * SUBLANES, SUBLANES), SUBLANES), :],
            xs_ref.at[pl.ds(pl.multiple_of(slot * SUBLANES, SUBLANES), SUBLANES), :],
            sem)

    def chunk_wait():
        pltpu.make_async_copy(
            h_ref.at[pl.ds(0, 2 * CH * SUBLANES), :], xs_ref.at[pl.ds(0, 2 * CH * SUBLANES), :], sem).wait()

    def issue(c, _):
        def one(j, _):
            t = c * CH + j
            row_copy(t, pos_ref[t]).start()
            row_copy(t, pos_ref[T + t]).start()
            return 0

        lax.fori_loop(0, CH, one, 0, unroll=8)

        @pl.when(c > 0)
        def _():
            chunk_wait()

        return 0

    zero_ref[...] = jnp.zeros_like(zero_ref)
    lax.fori_loop(0, nchunk, issue, 0)
    chunk_wait()

    def zero_copy(slot):
        return pltpu.make_async_copy(
            zero_ref.at[pl.ds(0, SUBLANES), :],
            xs_ref.at[pl.ds(pl.multiple_of(slot * SUBLANES, SUBLANES), SUBLANES), :], zsem)

    def tile_zero(j):
        return pltpu.make_async_copy(
            zero_ref, xs_ref.at[pl.ds(pl.multiple_of(j * tile_rows, tile_rows), tile_rows), :], zsem)

    def tz_start(j, _):
        tile_zero(j).start()
        return 0

    def tz_wait(j, _):
        tile_zero(j).wait()
        return 0

    lax.fori_loop(nt_ref[0], ntl, tz_start, 0)
    lax.fori_loop(nt_ref[0], ntl, tz_wait, 0)

    for e in range(N_EXPERTS):
        start = base_ref[e] + cnt_ref[e]
        npad = (TM_MOE - cnt_ref[e] % TM_MOE) % TM_MOE

        def zstart(r, _, start=start):
            zero_copy(start + r).start()
            return 0

        def zwait(r, _, start=start):
            zero_copy(start + r).wait()
            return 0

        lax.fori_loop(0, npad, zstart, 0)
        lax.fori_loop(0, npad, zwait, 0)


def _dispatch(pos_flat, base, cnt, ntiles, ht, nslots):
    return pl.pallas_call(
        _dispatch_kernel,
        grid_spec=pltpu.PrefetchScalarGridSpec(
            num_scalar_prefetch=4,
            grid=(1,),
            in_specs=[pl.BlockSpec(memory_space=pl.ANY)],
            out_specs=pl.BlockSpec(memory_space=pl.ANY),
            scratch_shapes=[
                pltpu.VMEM((TM_MOE * SUBLANES, LANES), F32),
                pltpu.SemaphoreType.DMA(()),
                pltpu.SemaphoreType.DMA(()),
            ],
        ),
        out_shape=jax.ShapeDtypeStruct((nslots * SUBLANES, LANES), F32),
        compiler_params=_cparams(("arbitrary",)),
        name="dispatch",
    )(pos_flat, base, cnt, ntiles, ht)


def _moe_kernel(texp_ref, nt_ref, xs_ref, wg_ref, wu_ref, wd_ref, ys_ref):
    j = pl.program_id(0)

    @pl.when(j < nt_ref[0])
    def _():
        TM = TM_MOE
        x = jnp.concatenate(
            [xs_ref[pl.ds(s, TM, stride=SUBLANES), :] for s in range(D_MODEL // LANES)], axis=1).astype(BF16)
        g = jnp.dot(x, wg_ref[0], preferred_element_type=F32)
        u = jnp.dot(x, wu_ref[0], preferred_element_type=F32)
        a = (g * jax.nn.sigmoid(g) * u).astype(BF16)
        y = jnp.dot(a, wd_ref[0], preferred_element_type=F32)
        for s in range(D_MODEL // LANES):
            ys_ref[pl.ds(s, TM, stride=SUBLANES), :] = y[:, s * LANES:(s + 1) * LANES]

    @pl.when(j >= nt_ref[0])
    def _():
        ys_ref[...] = jnp.zeros_like(ys_ref)


def _moe(texp, ntiles, xs, wg, wu, wd, ntl):
    rows = TM_MOE * SUBLANES

    def row_map(j, texp, nt):
        return (jnp.minimum(j, nt[0] - 1), 0)

    def w_map(j, texp, nt):
        return (texp[jnp.minimum(j, nt[0] - 1)], 0, 0)

    return pl.pallas_call(
        _moe_kernel,
        grid_spec=pltpu.PrefetchScalarGridSpec(
            num_scalar_prefetch=2,
            grid=(ntl,),
            in_specs=[
                pl.BlockSpec((rows, LANES), row_map),
                pl.BlockSpec((1, D_MODEL, EXPERT_FF), w_map),
                pl.BlockSpec((1, D_MODEL, EXPERT_FF), w_map),
                pl.BlockSpec((1, EXPERT_FF, D_MODEL), w_map),
            ],
            out_specs=pl.BlockSpec((rows, LANES), lambda j, texp, nt: (j, 0)),
        ),
        out_shape=jax.ShapeDtypeStruct(xs.shape, F32),
        compiler_params=_cparams(("arbitrary",)),
        name="moe_ffn",
    )(texp, ntiles, xs, wg, wu, wd)


def _combine_kernel(pos_ref, x_ref, route_ref, nf_ref, ys_ref, o_ref, buf_ref, sem, *, final):
    TM = TM_COMB
    T = pos_ref.shape[0] // 2
    i = pl.program_id(0)

    def row_copy(slot, dst):
        return pltpu.make_async_copy(
            ys_ref.at[pl.ds(pl.multiple_of(slot * SUBLANES, SUBLANES), SUBLANES), :],
            buf_ref.at[pl.ds(pl.multiple_of(dst * SUBLANES, SUBLANES), SUBLANES), :],
            sem)

    def one(j, _):
        t = i * TM + j
        row_copy(pos_ref[t], j).start()
        row_copy(pos_ref[T + t], TM + j).start()
        return 0

    lax.fori_loop(0, TM, one, 0, unroll=8)
    pltpu.make_async_copy(ys_ref.at[pl.ds(0, 2 * TM * SUBLANES), :], buf_ref, sem).wait()

    r = route_ref[...]
    w1 = r[:, 0:1]
    w2 = r[:, 1:2]
    chunks = []
    for s in range(D_MODEL // LANES):
        y1 = buf_ref[pl.ds(s, TM, stride=SUBLANES), :]
        y2 = buf_ref[pl.ds(TM * SUBLANES + s, TM, stride=SUBLANES), :]
        chunks.append(x_ref[:, s * LANES:(s + 1) * LANES] + w1 * y1 + w2 * y2)
    xn = jnp.concatenate(chunks, axis=1)
    if final:
        xn = _rms(xn, nf_ref[...])
    o_ref[...] = xn


def _combine(pos_flat, x, route, norm_final, ys, final):
    T, D = x.shape
    TM = TM_COMB
    kernel = functools.partial(_combine_kernel, final=final)
    return pl.pallas_call(
        kernel,
        grid_spec=pltpu.PrefetchScalarGridSpec(
            num_scalar_prefetch=1,
            grid=(T // TM,),
            in_specs=[
                pl.BlockSpec((TM, D), lambda i, p: (i, 0)),
                pl.BlockSpec((TM, ROUTE_LANES), lambda i, p: (i, 0)),
                pl.BlockSpec((1, D), lambda i, p: (0, 0)),
                pl.BlockSpec(memory_space=pl.ANY),
            ],
            out_specs=pl.BlockSpec((TM, D), lambda i, p: (i, 0)),
            scratch_shapes=[
                pltpu.VMEM((2 * TM * SUBLANES, LANES), F32),
                pltpu.SemaphoreType.DMA(()),
            ],
        ),
        out_shape=jax.ShapeDtypeStruct((T, D), F32),
        compiler_params=_cparams(("arbitrary",)),
        name="combine",
    )(pos_flat, x, route, norm_final.reshape(1, D), ys)


def _tri_lower_strict(n):
    r = np.arange(n)
    return jnp.asarray((r[:, None] > r[None, :]).astype(np.float32), dtype=BF16)


def _tri_upper_strict(n):
    r = np.arange(n)
    return jnp.asarray((r[:, None] < r[None, :]).astype(np.float32), dtype=BF16)


def _block_diag_pool(w_pool):
    g, c, _ = w_pool.shape
    out = jnp.zeros((g * c, g * c), w_pool.dtype)
    for k in range(g):
        out = out.at[k * c:(k + 1) * c, k * c:(k + 1) * c].set(w_pool[k])
    return out


def kernel(x, mem, norm_mix, norm_mem, w_in, w_pool, pool_scale, w_kv_mem, w_up_sb, w_up_pool, w_up_x, w_out,
           norm_ffn, w_group, b_group, w_router, b_router, w_gate, w_up, w_down, norm_final):
    B, S, D = x.shape
    L = w_in.shape[0]
    T = B * S
    assert D == D_MODEL and S % T_ATT == 0 and S % TM_MERGE == 0 and T % TM_PROJ == 0 and T % TC_POS == 0

    ntl = 2 * T // TM_MOE + N_EXPERTS
    ntl_pad = -(-ntl // LANES) * LANES
    nslots = ntl * TM_MOE

    colscale = jnp.ones((1, IN_COLS), F32).at[:, :SB_WIDTH].set(HEAD_DIM ** -0.5 * LOG2E)
    tri_att = _tri_lower_strict(T_ATT)
    tri_pos = _tri_upper_strict(TC_POS)

    colscale = colscale.at[:, 3 * SB_WIDTH + POOL_WIDTH:3 * SB_WIDTH + POOL_WIDTH + X_WIDTH].set(HEAD_DIM ** -0.5)

    kv_all = _memkv(mem, norm_mem, w_kv_mem)
    xt = x.reshape(T, D)
    for l in range(L):
        z = _inproj(xt, norm_mix[l], w_in[l], colscale)
        sb = _sb_attention(z, tri_att, B, S)

        w_rt = jnp.zeros((D, ROUTE_LANES), F32)
        w_rt = w_rt.at[:, :N_GROUPS].set(w_group[l]).at[:, N_GROUPS:N_GROUPS + N_EXPERTS].set(w_router[l])
        w_rt_hi = w_rt.astype(BF16)
        w_rt_lo = (w_rt - w_rt_hi.astype(F32)).astype(BF16)
        b_rt = jnp.zeros((1, ROUTE_LANES), F32)
        b_rt = b_rt.at[0, :N_GROUPS].set(b_group[l]).at[0, N_GROUPS:N_GROUPS + N_EXPERTS].set(b_router[l])
        wts = (
            _block_diag_pool(w_pool[l]).astype(BF16),
            pool_scale[l].reshape(1, POOL_WIDTH),
            w_up_sb[l].astype(BF16),
            w_up_pool[l].astype(BF16),
            w_up_x[l].astype(BF16),
            w_out[l].astype(BF16),
            norm_ffn[l].reshape(1, D),
            w_rt_hi,
            w_rt_lo,
            b_rt,
        )
        x_mid, ht, route = _merge(xt, z, sb, kv_all, l, wts, B, S)

        e_idx = route[:, 2:4].T.astype(jnp.int32)
        pos, tile_info, cb = _positions(e_idx, tri_pos, ntl_pad)
        pos_flat = pos.reshape(2 * T)
        texp = tile_info[0, :ntl]
        ntiles = tile_info[1, :1]
        xs = _dispatch(pos_flat, cb[:, 0], cb[:, 1], ntiles, ht, nslots)
        ys = _moe(texp, ntiles, xs, w_gate[l].astype(BF16), w_up[l].astype(BF16), w_down[l].astype(BF16), ntl)
        xt = _combine(pos_flat, x_mid, route, norm_final, ys, final=(l == L - 1))
    return xt.reshape(B, S, D)
```

```python
import functools
import math

import jax
import jax.numpy as jnp
import numpy as np
from jax import lax
from jax.experimental import pallas as pl
from jax.experimental.pallas import tpu as pltpu

F32 = jnp.float32
BF16 = jnp.bfloat16

D_MODEL = 1024
HEAD_DIM = 64
SB_HEADS = 8
SB_WIDTH = SB_HEADS * HEAD_DIM
POOL_WINDOWS = (2, 4, 8, 16)
POOL_GROUP_DIM = 64
POOL_WIDTH = 256
X_HEADS = 4
X_WIDTH = X_HEADS * HEAD_DIM
N_BRANCH = 3
IN_COLS = 3 * SB_WIDTH + POOL_WIDTH + X_WIDTH + N_BRANCH * D_MODEL
N_GROUPS = 4
EXPERTS_PER_GROUP = 4
N_EXPERTS = N_GROUPS * EXPERTS_PER_GROUP
EXPERT_FF = 512
EPS = 1e-6

LANES = 128
SUBLANES = 8
VMEM_LIMIT = 56 * 1024 * 1024

TM_PROJ = 1024
TN_PROJ = 1024
T_ATT = 256
TM_MERGE = 256
TM_MOE = 256
TM_COMB = 256
TC_POS = 512
HALO = 16

LOG2E = 1.4426950408889634
INV_LN2 = LOG2E
NEG_BIG = -1e30
ROUTE_LANES = LANES


def _cparams(sem):
    return pltpu.CompilerParams(dimension_semantics=sem, vmem_limit_bytes=VMEM_LIMIT)


def _rms(x, g):
    ms = jnp.mean(x * x, axis=-1, keepdims=True)
    return x * lax.rsqrt(ms + EPS) * g


def _memkv_kernel(mem_ref, g_ref, w_ref, kv_ref):
    m = _rms(mem_ref[0], g_ref[0]).astype(BF16)
    kv_ref[0, 0] = jnp.dot(m, w_ref[0].astype(BF16), preferred_element_type=F32).astype(BF16)


def _memkv(mem, norm_mem, w_kv_mem):
    L = norm_mem.shape[0]
    B, M, D = mem.shape
    return pl.pallas_call(
        _memkv_kernel,
        grid=(L, B),
        in_specs=[
            pl.BlockSpec((1, M, D), lambda l, b: (b, 0, 0)),
            pl.BlockSpec((1, 1, D), lambda l, b: (l, 0, 0)),
            pl.BlockSpec((1, D, 2 * X_WIDTH), lambda l, b: (l, 0, 0)),
        ],
        out_specs=pl.BlockSpec((1, 1, M, 2 * X_WIDTH), lambda l, b: (l, b, 0, 0)),
        out_shape=jax.ShapeDtypeStruct((L, B, M, 2 * X_WIDTH), BF16),
        compiler_params=_cparams(("arbitrary", "arbitrary")),
        name="memkv",
    )(mem, norm_mem.reshape(L, 1, D), w_kv_mem)


def _inproj_kernel(x_ref, g_ref, w_ref, cs_ref, z_ref, h_ref):
    @pl.when(pl.program_id(1) == 0)
    def _():
        h_ref[...] = _rms(x_ref[...], g_ref[...]).astype(BF16)

    acc = jnp.dot(h_ref[...], w_ref[...].astype(BF16), preferred_element_type=F32)
    z_ref[...] = (acc * cs_ref[...]).astype(BF16)


def _inproj(x, g, w, colscale):
    T, D = x.shape
    N = w.shape[1]
    return pl.pallas_call(
        _inproj_kernel,
        grid=(T // TM_PROJ, N // TN_PROJ),
        in_specs=[
            pl.BlockSpec((TM_PROJ, D), lambda i, j: (i, 0)),
            pl.BlockSpec((1, D), lambda i, j: (0, 0)),
            pl.BlockSpec((D, TN_PROJ), lambda i, j: (0, j)),
            pl.BlockSpec((1, TN_PROJ), lambda i, j: (0, j)),
        ],
        out_specs=pl.BlockSpec((TM_PROJ, TN_PROJ), lambda i, j: (i, j)),
        out_shape=jax.ShapeDtypeStruct((T, N), BF16),
        scratch_shapes=[pltpu.VMEM((TM_PROJ, D), BF16)],
        compiler_params=_cparams(("arbitrary", "arbitrary")),
        name="inproj",
    )(x, g.reshape(1, D), w, colscale)


def _sb_attn_kernel(q_ref, k_ref, v_ref, tri_ref, o_ref, acc_ref, c_ref):
    S = q_ref.shape[0]
    T = T_ATT
    nq = S // T
    lane = lax.broadcasted_iota(jnp.int32, (T, LANES), 1)
    row = lax.broadcasted_iota(jnp.int32, (2 * T, T), 0)
    col = lax.broadcasted_iota(jnp.int32, (2 * T, T), 1)
    strict = col < jnp.where(row >= T, row - T, row)
    tri = tri_ref[...]

    def q_block(qi, _):
        q = q_ref[pl.ds(pl.multiple_of(qi * T, T), T), :]
        zero = jnp.zeros_like(q)
        q2 = jnp.concatenate([jnp.where(lane < HEAD_DIM, q, zero), jnp.where(lane >= HEAD_DIM, q, zero)], axis=0)

        def scores(kb):
            kblk = k_ref[pl.ds(pl.multiple_of(kb * T, T), T), :]
            z2 = lax.dot_general(q2, kblk, (((1,), (1,)), ((), ())), preferred_element_type=F32)
            u = jnp.exp2(jnp.minimum(z2, 43.0))
            sp = jnp.maximum(jnp.log(1.0 + u) * INV_LN2, z2)
            return z2, sp

        z2, sp = scores(qi)
        l = jnp.where(strict, sp, 0.0).astype(BF16)
        cum = jnp.dot(l, tri, preferred_element_type=F32)
        a = jnp.where(strict, jnp.exp2(z2 - sp - cum), 0.0).astype(BF16)
        vblk = v_ref[pl.ds(pl.multiple_of(qi * T, T), T), :]
        acc_ref[...] = jnp.dot(a, vblk, preferred_element_type=F32)
        c_ref[...] = cum[:, 0:1] + l[:, 0:1].astype(F32)

        def k_block(i, _):
            kb = qi - 1 - i
            z2, sp = scores(kb)
            l = sp.astype(BF16)
            cum = jnp.dot(l, tri, preferred_element_type=F32)
            c = c_ref[...]
            a = jnp.exp2(z2 - sp - cum - c).astype(BF16)
            vblk = v_ref[pl.ds(pl.multiple_of(kb * T, T), T), :]
            acc_ref[...] += jnp.dot(a, vblk, preferred_element_type=F32)
            c_ref[...] = c + cum[:, 0:1] + l[:, 0:1].astype(F32)
            return 0

        lax.fori_loop(0, qi, k_block, 0)
        acc = acc_ref[...]
        o_ref[pl.ds(pl.multiple_of(qi * T, T), T), :] = jnp.where(lane < HEAD_DIM, acc[:T], acc[T:]).astype(o_ref.dtype)
        return 0

    lax.fori_loop(0, nq, q_block, 0)


def _sb_attention(z, tri, batch, seq):
    npair = SB_WIDTH // LANES
    return pl.pallas_call(
        _sb_attn_kernel,
        grid=(batch, npair),
        in_specs=[
            pl.BlockSpec((seq, LANES), lambda b, p: (b, p)),
            pl.BlockSpec((seq, LANES), lambda b, p: (b, npair + p)),
            pl.BlockSpec((seq, LANES), lambda b, p: (b, 2 * npair + p)),
            pl.BlockSpec((T_ATT, T_ATT), lambda b, p: (0, 0)),
        ],
        out_specs=pl.BlockSpec((seq, LANES), lambda b, p: (b, p)),
        out_shape=jax.ShapeDtypeStruct((batch * seq, SB_WIDTH), BF16),
        scratch_shapes=[pltpu.VMEM((2 * T_ATT, LANES), F32), pltpu.VMEM((2 * T_ATT, 1), F32)],
        compiler_params=_cparams(("arbitrary", "arbitrary")),
        name="sb_attention",
    )(z, z, z, tri)


def _merge_kernel(
    x_ref, u_ref, up_ref, qx_ref, g0_ref, g1_ref, g2_ref, sb_ref, kv_ref,
    wpool_ref, pscale_ref, wsb_ref, wpl_ref, wxa_ref, wout_ref, nffn_ref, wrh_ref, wrl_ref, brt_ref,
    xo_ref, ht_ref, route_ref, ubuf_ref, *, tiles_per_seq):
    TM = TM_MERGE
    i = pl.program_id(0)
    first = (i % tiles_per_seq) == 0

    u = u_ref[...].astype(F32)
    hist = up_ref[TM - HALO:, :].astype(F32)
    ubuf_ref[0:HALO, :] = hist * jnp.where(first, 0.0, 1.0)
    ubuf_ref[HALO:, :] = u
    lane_p = lax.broadcasted_iota(jnp.int32, (TM, POOL_WIDTH), 1)
    pos = (lax.broadcasted_iota(jnp.int32, (TM, POOL_WIDTH), 0) + (i % tiles_per_seq) * TM + 1).astype(F32)

    def shifted(k):
        return ubuf_ref[HALO - k:HALO - k + TM, :]

    s2 = u + shifted(1)
    s4 = s2 + shifted(2) + shifted(3)
    s8 = s4
    for k in range(4, 8):
        s8 = s8 + shifted(k)
    s16 = s8
    for k in range(8, 16):
        s16 = s16 + shifted(k)
    grp = lax.shift_right_logical(lane_p, int(math.log2(POOL_GROUP_DIM)))
    wsum = jnp.where(grp == 0, s2, jnp.where(grp == 1, s4, jnp.where(grp == 2, s8, s16)))
    win = jnp.where(grp == 0, 2.0, jnp.where(grp == 1, 4.0, jnp.where(grp == 2, 8.0, 16.0)))
    d = wsum / jnp.minimum(pos, win) - u
    yp = jnp.dot(d.astype(BF16), wpool_ref[...], preferred_element_type=F32) * pscale_ref[...]
    pl_up = jnp.dot(yp.astype(BF16), wpl_ref[...], preferred_element_type=F32)

    qx = qx_ref[...]
    kv = kv_ref[0, 0]
    kmem = kv[:, :X_WIDTH]
    vmem = kv[:, X_WIDTH:]
    lane_x = lax.broadcasted_iota(jnp.int32, (TM, X_WIDTH), 1)
    xa = jnp.zeros((TM, X_WIDTH), F32)
    for h in range(X_HEADS):
        hm = (lane_x >= h * HEAD_DIM) & (lane_x < (h + 1) * HEAD_DIM)
        qh = jnp.where(hm, qx, jnp.zeros_like(qx))
        s = lax.dot_general(qh, kmem, (((1,), (1,)), ((), ())), preferred_element_type=F32)
        m = jnp.max(s, axis=-1, keepdims=True)
        p = jnp.exp(s - m)
        den = jnp.sum(p, axis=-1, keepdims=True)
        o = jnp.dot(p.astype(BF16), vmem, preferred_element_type=F32)
        xa = jnp.where(hm, o / den, xa)
    xa_up = jnp.dot(xa.astype(BF16), wxa_ref[...], preferred_element_type=F32)

    sb_up = jnp.dot(sb_ref[...], wsb_ref[...], preferred_element_type=F32)
    merged = (jax.nn.sigmoid(g0_ref[...].astype(F32)) * sb_up
              + jax.nn.sigmoid(g1_ref[...].astype(F32)) * pl_up
              + jax.nn.sigmoid(g2_ref[...].astype(F32)) * xa_up)
    xn = x_ref[...] + jnp.dot(merged.astype(BF16), wout_ref[...], preferred_element_type=F32)
    xo_ref[...] = xn

    h2 = _rms(xn, nffn_ref[...])
    for s in range(D_MODEL // LANES):
        ht_ref[pl.ds(s, TM, stride=SUBLANES), :] = h2[:, s * LANES:(s + 1) * LANES]

    hh = h2.astype(BF16)
    hl = (h2 - hh.astype(F32)).astype(BF16)
    wrh = wrh_ref[...]
    logits = (jnp.dot(hh, wrh, preferred_element_type=F32)
              + jnp.dot(hl, wrh, preferred_element_type=F32)
              + jnp.dot(hh, wrl_ref[...], preferred_element_type=F32)) + brt_ref[...]
    lane = lax.broadcasted_iota(jnp.int32, (TM, ROUTE_LANES), 1)
    lanef = lane.astype(F32)
    big = float(ROUTE_LANES)
    glm = jnp.where(lane < N_GROUPS, logits, NEG_BIG)
    gmax = jnp.max(glm, axis=-1, keepdims=True)
    gidx = jnp.min(jnp.where(glm == gmax, lanef, big), axis=-1, keepdims=True)
    gprob = 1.0 / jnp.sum(jnp.exp(glm - gmax), axis=-1, keepdims=True)
    lo = N_GROUPS + EXPERTS_PER_GROUP * gidx
    elm = jnp.where((lanef >= lo) & (lanef < lo + EXPERTS_PER_GROUP), logits, NEG_BIG)
    v1 = jnp.max(elm, axis=-1, keepdims=True)
    i1 = jnp.min(jnp.where(elm == v1, lanef, big), axis=-1, keepdims=True)
    elm2 = jnp.where(lanef == i1, NEG_BIG, elm)
    v2 = jnp.max(elm2, axis=-1, keepdims=True)
    i2 = jnp.min(jnp.where(elm2 == v2, lanef, big), axis=-1, keepdims=True)
    e21 = jnp.exp(v2 - v1)
    w1 = gprob / (1.0 + e21)
    w2 = gprob * e21 / (1.0 + e21)
    route_ref[...] = jnp.where(lane == 0, w1, jnp.where(lane == 1, w2, jnp.where(
        lane == 2, i1 - N_GROUPS, jnp.where(lane == 3, i2 - N_GROUPS, 0.0))))


def _merge(x, z, sb, kv, layer, wts, batch, seq):
    T, D = x.shape
    TM = TM_MERGE
    tiles_per_seq = seq // TM
    ucol = (3 * SB_WIDTH) // POOL_WIDTH
    qxcol = (3 * SB_WIDTH + POOL_WIDTH) // X_WIDTH
    gcol = (3 * SB_WIDTH + POOL_WIDTH + X_WIDTH) // D_MODEL

    def const(shape):
        return pl.BlockSpec(shape, lambda i: (0,) * len(shape))

    kernel = functools.partial(_merge_kernel, tiles_per_seq=tiles_per_seq)
    return pl.pallas_call(
        kernel,
        grid=(T // TM,),
        in_specs=[
            pl.BlockSpec((TM, D), lambda i: (i, 0)),
            pl.BlockSpec((TM, POOL_WIDTH), lambda i: (i, ucol)),
            pl.BlockSpec((TM, POOL_WIDTH), lambda i: (jnp.maximum(i - 1, 0), ucol)),
            pl.BlockSpec((TM, X_WIDTH), lambda i: (i, qxcol)),
            pl.BlockSpec((TM, D), lambda i: (i, gcol)),
            pl.BlockSpec((TM, D), lambda i: (i, gcol + 1)),
            pl.BlockSpec((TM, D), lambda i: (i, gcol + 2)),
            pl.BlockSpec((TM, SB_WIDTH), lambda i: (i, 0)),
            pl.BlockSpec((1, 1) + kv.shape[2:], lambda i: (layer, i // tiles_per_seq, 0, 0)),
            const((POOL_WIDTH, POOL_WIDTH)),
            const((1, POOL_WIDTH)),
            const((SB_WIDTH, D)),
            const((POOL_WIDTH, D)),
            const((X_WIDTH, D)),
            const((D, D)),
            const((1, D)),
            const((D, ROUTE_LANES)),
            const((D, ROUTE_LANES)),
            const((1, ROUTE_LANES)),
        ],
        out_specs=[
            pl.BlockSpec((TM, D), lambda i: (i, 0)),
            pl.BlockSpec((TM * SUBLANES, LANES), lambda i: (i, 0)),
            pl.BlockSpec((TM, ROUTE_LANES), lambda i: (i, 0)),
        ],
        out_shape=[
            jax.ShapeDtypeStruct((T, D), F32),
            jax.ShapeDtypeStruct((T * SUBLANES, LANES), F32),
            jax.ShapeDtypeStruct((T, ROUTE_LANES), F32),
        ],
        scratch_shapes=[pltpu.VMEM((HALO + TM, POOL_WIDTH), F32)],
        compiler_params=_cparams(("arbitrary",)),
        name="merge",
    )(x, z, z, z, z, z, z, sb, kv, *wts)


def _pos_kernel(e_ref, tri_ref, pos_ref, tile_ref, cb_ref, carry_ref):
    T = e_ref.shape[1]
    TC = TC_POS
    nchunk = T // TC
    eid = lax.broadcasted_iota(jnp.int32, (N_EXPERTS, TC), 0)
    tri = tri_ref[...]
    carry_ref[...] = jnp.zeros_like(carry_ref)

    def rank_chunk(c, _):
        off = pl.multiple_of(c * TC, TC)
        e0 = e_ref[0:1, pl.ds(off, TC)]
        e1 = e_ref[1:2, pl.ds(off, TC)]
        m0 = eid == e0
        m1 = eid == e1
        m = jnp.where(m0 | m1, 1.0, 0.0)
        cum = jnp.dot(m.astype(BF16), tri, preferred_element_type=F32) + carry_ref[...]
        r0 = jnp.sum(jnp.where(m0, cum, 0.0), axis=0, keepdims=True)
        r1 = jnp.sum(jnp.where(m1, cum, 0.0), axis=0, keepdims=True)
        pos_ref[0:1, pl.ds(off, TC)] = r0.astype(jnp.int32)
        pos_ref[1:2, pl.ds(off, TC)] = r1.astype(jnp.int32)
        carry_ref[...] = carry_ref[...] + jnp.sum(m, axis=1, keepdims=True)
        return 0

    lax.fori_loop(0, nchunk, rank_chunk, 0)

    cnt = carry_ref[...]
    ntile = jnp.floor((cnt + (TM_MOE - 1)) * (1.0 / TM_MOE))
    er = lax.broadcasted_iota(jnp.int32, (N_EXPERTS, N_EXPERTS), 0)
    ec = lax.broadcasted_iota(jnp.int32, (N_EXPERTS, N_EXPERTS), 1)
    lower = jnp.where(ec < er, 1.0, 0.0).astype(BF16)
    ntile_b = jnp.broadcast_to(ntile, (N_EXPERTS, LANES)).astype(BF16)
    base_t = jnp.dot(lower, ntile_b, preferred_element_type=F32)
    base = base_t[:, 0:1] * float(TM_MOE)
    end_t = base_t[:, 0:1] + ntile

    def add_base(c, _):
        off = pl.multiple_of(c * TC, TC)
        e0 = e_ref[0:1, pl.ds(off, TC)]
        e1 = e_ref[1:2, pl.ds(off, TC)]
        b0 = jnp.sum(jnp.where(eid == e0, base, 0.0), axis=0, keepdims=True)
        b1 = jnp.sum(jnp.where(eid == e1, base, 0.0), axis=0, keepdims=True)
        pos_ref[0:1, pl.ds(off, TC)] = pos_ref[0:1, pl.ds(off, TC)] + b0.astype(jnp.int32)
        pos_ref[1:2, pl.ds(off, TC)] = pos_ref[1:2, pl.ds(off, TC)] + b1.astype(jnp.int32)
        return 0

    lax.fori_loop(0, nchunk, add_base, 0)

    ntl = tile_ref.shape[1]
    tidx = lax.broadcasted_iota(jnp.int32, (N_EXPERTS, ntl), 1).astype(F32)
    texp = jnp.sum(jnp.where(tidx >= end_t, 1.0, 0.0), axis=0, keepdims=True)
    total = jnp.max(end_t, axis=0, keepdims=True)
    rowi = lax.broadcasted_iota(jnp.int32, tile_ref.shape, 0)
    tile_ref[...] = jnp.where(rowi == 0, jnp.minimum(texp, float(N_EXPERTS - 1)), total).astype(jnp.int32)
    lane = lax.broadcasted_iota(jnp.int32, cb_ref.shape, 1)
    cb_ref[...] = jnp.where(lane == 0, base, jnp.where(lane == 1, cnt, 0.0)).astype(jnp.int32)


def _positions(e_idx, tri, ntl):
    T = e_idx.shape[1]
    return pl.pallas_call(
        _pos_kernel,
        grid=(1,),
        in_specs=[
            pl.BlockSpec((2, T), lambda i: (0, 0)),
            pl.BlockSpec((TC_POS, TC_POS), lambda i: (0, 0)),
        ],
        out_specs=[
            pl.BlockSpec((2, T), lambda i: (0, 0)),
            pl.BlockSpec((SUBLANES, ntl), lambda i: (0, 0)),
            pl.BlockSpec((N_EXPERTS, LANES), lambda i: (0, 0)),
        ],
        out_shape=[
            jax.ShapeDtypeStruct((2, T), jnp.int32),
            jax.ShapeDtypeStruct((SUBLANES, ntl), jnp.int32),
            jax.ShapeDtypeStruct((N_EXPERTS, LANES), jnp.int32),
        ],
        scratch_shapes=[pltpu.VMEM((N_EXPERTS, 1), F32)],
        compiler_params=_cparams(("arbitrary",)),
        name="positions",
    )(e_idx, tri)


TM_DISP = 1024


def _dispatch_kernel(pos_ref, base_ref, cnt_ref, nt_ref, h_ref, xs_ref, zero_ref, sem, zsem):
    T = pos_ref.shape[0] // 2
    TM = TM_DISP
    i = pl.program_id(0)
    tile_rows = TM_MOE * SUBLANES
    ntl = xs_ref.shape[0] // tile_rows

    def row_copy(j, slot):
        return pltpu.make_async_copy(
            h_ref.at[pl.ds(pl.multiple_of(j * SUBLANES, SUBLANES), SUBLANES), :],
            xs_ref.at[pl.ds(pl.multiple_of(slot * SUBLANES, SUBLANES), SUBLANES), :],
            sem)

    def one(j, _):
        t = i * TM + j
        row_copy(j, pos_ref[t]).start()
        row_copy(j, pos_ref[T + t]).start()
        return 0

    lax.fori_loop(0, TM, one, 0, unroll=8)
    for _ in range(2):
        pltpu.make_async_copy(h_ref, xs_ref.at[pl.ds(0, TM * SUBLANES), :], sem).wait()

    @pl.when(i == pl.num_programs(0) - 1)
    def _():
        zero_ref[...] = jnp.zeros_like(zero_ref)

        def zero_copy(slot):
            return pltpu.make_async_copy(
                zero_ref.at[pl.ds(0, SUBLANES), :],
                xs_ref.at[pl.ds(pl.multiple_of(slot * SUBLANES, SUBLANES), SUBLANES), :], zsem)

        def tile_zero(j):
            return pltpu.make_async_copy(
                zero_ref, xs_ref.at[pl.ds(pl.multiple_of(j * tile_rows, tile_rows), tile_rows), :], zsem)

        def tz_start(j, _):
            tile_zero(j).start()
            return 0

        def tz_wait(j, _):
            tile_zero(j).wait()
            return 0

        lax.fori_loop(nt_ref[0], ntl, tz_start, 0)
        lax.fori_loop(nt_ref[0], ntl, tz_wait, 0)

        for e in range(N_EXPERTS):
            start = base_ref[e] + cnt_ref[e]
            npad = (TM_MOE - cnt_ref[e] % TM_MOE) % TM_MOE

            def zstart(r, _, start=start):
                zero_copy(start + r).start()
                return 0

            def zwait(r, _, start=start):
                zero_copy(start + r).wait()
                return 0

            lax.fori_loop(0, npad, zstart, 0)
            lax.fori_loop(0, npad, zwait, 0)


def _dispatch(pos_flat, base, cnt, ntiles, ht, nslots):
    T = pos_flat.shape[0] // 2
    return pl.pallas_call(
        _dispatch_kernel,
        grid_spec=pltpu.PrefetchScalarGridSpec(
            num_scalar_prefetch=4,
            grid=(T // TM_DISP,),
            in_specs=[pl.BlockSpec((TM_DISP * SUBLANES, LANES), lambda i, *_: (i, 0))],
            out_specs=pl.BlockSpec(memory_space=pl.ANY),
            scratch_shapes=[
                pltpu.VMEM((TM_MOE * SUBLANES, LANES), F32),
                pltpu.SemaphoreType.DMA(()),
                pltpu.SemaphoreType.DMA(()),
            ],
        ),
        out_shape=jax.ShapeDtypeStruct((nslots * SUBLANES, LANES), F32),
        compiler_params=_cparams(("arbitrary",)),
        name="dispatch",
    )(pos_flat, base, cnt, ntiles, ht)


def _moe_kernel(texp_ref, nt_ref, xs_ref, wg_ref, wu_ref, wd_ref, ys_ref):
    j = pl.program_id(0)

    @pl.when(j < nt_ref[0])
    def _():
        TM = TM_MOE
        x = jnp.concatenate(
            [xs_ref[pl.ds(s, TM, stride=SUBLANES), :] for s in range(D_MODEL // LANES)], axis=1).astype(BF16)
        g = jnp.dot(x, wg_ref[0], preferred_element_type=F32)
        u = jnp.dot(x, wu_ref[0], preferred_element_type=F32)
        a = (g * jax.nn.sigmoid(g) * u).astype(BF16)
        y = jnp.dot(a, wd_ref[0], preferred_element_type=F32)
        for s in range(D_MODEL // LANES):
            ys_ref[pl.ds(s, TM, stride=SUBLANES), :] = y[:, s * LANES:(s + 1) * LANES]

    @pl.when(j >= nt_ref[0])
    def _():
        ys_ref[...] = jnp.zeros_like(ys_ref)


def _moe(texp, ntiles, xs, wg, wu, wd, ntl):
    rows = TM_MOE * SUBLANES

    def row_map(j, texp, nt):
        return (jnp.minimum(j, nt[0] - 1), 0)

    def w_map(j, texp, nt):
        return (texp[jnp.minimum(j, nt[0] - 1)], 0, 0)

    return pl.pallas_call(
        _moe_kernel,
        grid_spec=pltpu.PrefetchScalarGridSpec(
            num_scalar_prefetch=2,
            grid=(ntl,),
            in_specs=[
                pl.BlockSpec((rows, LANES), row_map),
                pl.BlockSpec((1, D_MODEL, EXPERT_FF), w_map),
                pl.BlockSpec((1, D_MODEL, EXPERT_FF), w_map),
                pl.BlockSpec((1, EXPERT_FF, D_MODEL), w_map),
            ],
            out_specs=pl.BlockSpec((rows, LANES), lambda j, texp, nt: (j, 0)),
        ),
        out_shape=jax.ShapeDtypeStruct(xs.shape, F32),
        compiler_params=_cparams(("arbitrary",)),
        name="moe_ffn",
    )(texp, ntiles, xs, wg, wu, wd)


def _combine_kernel(pos_ref, x_ref, route_ref, nf_ref, ys_ref, o_ref, buf_ref, sems, *, final):
    TM = TM_COMB
    T = pos_ref.shape[0] // 2
    i = pl.program_id(0)
    n = pl.num_programs(0)
    half = 2 * TM * SUBLANES

    def gather(step, slot):
        base = slot * half

        def one(j, _):
            t = step * TM + j
            for k in range(2):
                src = pos_ref[k * T + t]
                pltpu.make_async_copy(
                    ys_ref.at[pl.ds(pl.multiple_of(src * SUBLANES, SUBLANES), SUBLANES), :],
                    buf_ref.at[pl.ds(pl.multiple_of(base + (k * TM + j) * SUBLANES, SUBLANES), SUBLANES), :],
                    sems.at[slot]).start()
            return 0

        lax.fori_loop(0, TM, one, 0, unroll=8)

    slot = lax.rem(i, 2)

    @pl.when(i == 0)
    def _():
        gather(0, 0)

    @pl.when(i + 1 < n)
    def _():
        gather(i + 1, 1 - slot)

    base = pl.multiple_of(slot * half, half)
    pltpu.make_async_copy(ys_ref.at[pl.ds(0, half), :], buf_ref.at[pl.ds(base, half), :], sems.at[slot]).wait()

    r = route_ref[...]
    w1 = r[:, 0:1]
    w2 = r[:, 1:2]
    chunks = []
    for s in range(D_MODEL // LANES):
        y1 = buf_ref[pl.ds(base + s, TM, stride=SUBLANES), :]
        y2 = buf_ref[pl.ds(base + TM * SUBLANES + s, TM, stride=SUBLANES), :]
        chunks.append(x_ref[:, s * LANES:(s + 1) * LANES] + w1 * y1 + w2 * y2)
    xn = jnp.concatenate(chunks, axis=1)
    if final:
        xn = _rms(xn, nf_ref[...])
    o_ref[...] = xn


def _combine(pos_flat, x, route, norm_final, ys, final):
    T, D = x.shape
    TM = TM_COMB
    kernel = functools.partial(_combine_kernel, final=final)
    return pl.pallas_call(
        kernel,
        grid_spec=pltpu.PrefetchScalarGridSpec(
            num_scalar_prefetch=1,
            grid=(T // TM,),
            in_specs=[
                pl.BlockSpec((TM, D), lambda i, p: (i, 0)),
                pl.BlockSpec((TM, ROUTE_LANES), lambda i, p: (i, 0)),
                pl.BlockSpec((1, D), lambda i, p: (0, 0)),
                pl.BlockSpec(memory_space=pl.ANY),
            ],
            out_specs=pl.BlockSpec((TM, D), lambda i, p: (i, 0)),
            scratch_shapes=[
                pltpu.VMEM((2 * 2 * TM * SUBLANES, LANES), F32),
                pltpu.SemaphoreType.DMA((2,)),
            ],
        ),
        out_shape=jax.ShapeDtypeStruct((T, D), F32),
        compiler_params=_cparams(("arbitrary",)),
        name="combine",
    )(pos_flat, x, route, norm_final.reshape(1, D), ys)


def _tri_lower_strict(n):
    r = np.arange(n)
    return jnp.asarray((r[:, None] > r[None, :]).astype(np.float32), dtype=BF16)


def _tri_upper_strict(n):
    r = np.arange(n)
    return jnp.asarray((r[:, None] < r[None, :]).astype(np.float32), dtype=BF16)


def _block_diag_pool(w_pool):
    g, c, _ = w_pool.shape
    out = jnp.zeros((g * c, g * c), w_pool.dtype)
    for k in range(g):
        out = out.at[k * c:(k + 1) * c, k * c:(k + 1) * c].set(w_pool[k])
    return out


def kernel(x, mem, norm_mix, norm_mem, w_in, w_pool, pool_scale, w_kv_mem, w_up_sb, w_up_pool, w_up_x, w_out,
           norm_ffn, w_group, b_group, w_router, b_router, w_gate, w_up, w_down, norm_final):
    B, S, D = x.shape
    L = w_in.shape[0]
    T = B * S
    assert D == D_MODEL and S % T_ATT == 0 and S % TM_MERGE == 0 and T % TM_PROJ == 0 and T % TC_POS == 0
    assert T % TM_DISP == 0 and T % TM_COMB == 0

    ntl = 2 * T // TM_MOE + N_EXPERTS
    ntl_pad = -(-ntl // LANES) * LANES
    nslots = ntl * TM_MOE

    colscale = jnp.ones((1, IN_COLS), F32).at[:, :SB_WIDTH].set(HEAD_DIM ** -0.5 * LOG2E)
    tri_att = _tri_lower_strict(T_ATT)
    tri_pos = _tri_upper_strict(TC_POS)

    colscale = colscale.at[:, 3 * SB_WIDTH + POOL_WIDTH:3 * SB_WIDTH + POOL_WIDTH + X_WIDTH].set(HEAD_DIM ** -0.5)

    kv_all = _memkv(mem, norm_mem, w_kv_mem)
    xt = x.reshape(T, D)
    for l in range(L):
        z = _inproj(xt, norm_mix[l], w_in[l], colscale)
        sb = _sb_attention(z, tri_att, B, S)

        w_rt = jnp.zeros((D, ROUTE_LANES), F32)
        w_rt = w_rt.at[:, :N_GROUPS].set(w_group[l]).at[:, N_GROUPS:N_GROUPS + N_EXPERTS].set(w_router[l])
        w_rt_hi = w_rt.astype(BF16)
        w_rt_lo = (w_rt - w_rt_hi.astype(F32)).astype(BF16)
        b_rt = jnp.zeros((1, ROUTE_LANES), F32)
        b_rt = b_rt.at[0, :N_GROUPS].set(b_group[l]).at[0, N_GROUPS:N_GROUPS + N_EXPERTS].set(b_router[l])
        wts = (
            _block_diag_pool(w_pool[l]).astype(BF16),
            pool_scale[l].reshape(1, POOL_WIDTH),
            w_up_sb[l].astype(BF16),
            w_up_pool[l].astype(BF16),
            w_up_x[l].astype(BF16),
            w_out[l].astype(BF16),
            norm_ffn[l].reshape(1, D),
            w_rt_hi,
            w_rt_lo,
            b_rt,
        )
        x_mid, ht, route = _merge(xt, z, sb, kv_all, l, wts, B, S)

        e_idx = route[:, 2:4].T.astype(jnp.int32)
        pos, tile_info, cb = _positions(e_idx, tri_pos, ntl_pad)
        pos_flat = pos.reshape(2 * T)
        texp = tile_info[0, :ntl]
        ntiles = tile_info[1, :1]
        xs = _dispatch(pos_flat, cb[:, 0], cb[:, 1], ntiles, ht, nslots)
        ys = _moe(texp, ntiles, xs, w_gate[l].astype(BF16), w_up[l].astype(BF16), w_down[l].astype(BF16), ntl)
        xt = _combine(pos_flat, x_mid, route, norm_final, ys, final=(l == L - 1))
    return xt.reshape(B, S, D)
```
